```python
import jax, jax.numpy as jnp
from jax import lax
import numpy as np

D_MODEL = 2048
BATCH = 2
SEQ = 4096
DEPTH = 1

CHUNK = 64
SGU_BLOCK = 128
D_SGU = D_MODEL // 2
SGU_HEADS = 8
SGU_HEAD_DIM = D_SGU // SGU_HEADS
D_POOL = D_MODEL // 2
POOL_WINDOWS = (2, 4, 8, 16)
POOL_GROUPS = len(POOL_WINDOWS)
POOL_GROUP_DIM = D_POOL // POOL_GROUPS
D_FF = -(-8 * D_MODEL // (3 * 256)) * 256
D_IN = 2 * D_SGU + D_POOL + 2 * D_MODEL
EPS = 1e-6

kernel_name = "hybrid_sgu_pool_gated_block"


def rms_norm(x, g):
    xf = x.astype(jnp.float32)
    y = xf * lax.rsqrt(jnp.mean(xf * xf, axis=-1, keepdims=True) + EPS)
    return (y * g.astype(jnp.float32)).astype(x.dtype)


def layer_norm(x, g, b):
    xf = x.astype(jnp.float32)
    mu = jnp.mean(xf, axis=-1, keepdims=True)
    xc = xf - mu
    y = xc * lax.rsqrt(jnp.mean(xc * xc, axis=-1, keepdims=True) + EPS)
    return (y * g.astype(jnp.float32) + b.astype(jnp.float32)).astype(x.dtype)


def sgu_mixer(u, v, ln_g, ln_b, w_s, b_s):
    bsz, s, _ = v.shape
    nb = s // SGU_BLOCK
    v = layer_norm(v, ln_g, ln_b)
    idx = jnp.arange(SGU_BLOCK)
    mask = (idx[:, None] // CHUNK) >= (idx[None, :] // CHUNK)
    w = jnp.where(mask[None], w_s, 0)
    vb = v.reshape(bsz, nb, SGU_BLOCK, SGU_HEADS, SGU_HEAD_DIM)
    mixed = jnp.einsum('hij,bnjhd->bnihd', w, vb) + b_s.T[None, None, :, :, None]
    return u * mixed.reshape(bsz, s, D_SGU)


def pool_mixer(p, w_pool, scale):
    bsz, s, _ = p.shape
    pf = p.astype(jnp.float32)
    c = jnp.cumsum(pf, axis=1)
    count = jnp.arange(1, s + 1, dtype=jnp.float32)
    means = []
    for gi, w in enumerate(POOL_WINDOWS):
        cg = c[..., gi * POOL_GROUP_DIM:(gi + 1) * POOL_GROUP_DIM]
        lag = jnp.pad(cg[:, :s - w], ((0, 0), (w, 0), (0, 0)))
        means.append((cg - lag) / jnp.minimum(count, w)[None, :, None])
    pooled = (jnp.concatenate(means, axis=-1) - pf).astype(p.dtype)
    pooled = pooled.reshape(bsz, s, POOL_GROUPS, POOL_GROUP_DIM)
    y = jnp.einsum('bsgc,gcd->bsgd', pooled, w_pool).reshape(bsz, s, D_POOL)
    return y * scale


def setup_inputs(seed: int = 0) -> dict:
    key = jax.random.key(seed)
    ks = jax.random.split(key, 20)
    f32 = jnp.float32

    def nrm(k, shape, fan_in):
        return jax.random.normal(k, shape, f32) * (fan_in ** -0.5)

    def gain(k, shape):
        return 1.0 + 0.02 * jax.random.normal(k, shape, f32)

    L = DEPTH
    return {
        "x": jax.random.normal(ks[0], (BATCH, SEQ, D_MODEL), f32),
        "norm1_pre": gain(ks[1], (L, D_MODEL)),
        "w_in": nrm(ks[2], (L, D_MODEL, D_IN), D_MODEL),
        "v_ln_g": gain(ks[3], (L, D_SGU)),
        "v_ln_b": 0.02 * jax.random.normal(ks[4], (L, D_SGU), f32),
        "sgu_w": nrm(ks[5], (L, SGU_HEADS, SGU_BLOCK, SGU_BLOCK), SGU_BLOCK),
        "sgu_b": gain(ks[6], (L, SGU_HEADS, SGU_BLOCK)),
        "pool_w": nrm(ks[7], (L, POOL_GROUPS, POOL_GROUP_DIM, POOL_GROUP_DIM), POOL_GROUP_DIM),
        "pool_scale": gain(ks[8], (L, D_POOL)),
        "w_a_out": nrm(ks[9], (L, D_SGU, D_MODEL), D_SGU),
        "w_b_out": nrm(ks[10], (L, D_POOL, D_MODEL), D_POOL),
        "w_mix_out": nrm(ks[11], (L, D_MODEL, D_MODEL), D_MODEL),
        "norm1_post": gain(ks[12], (L, D_MODEL)),
        "norm2_pre": gain(ks[13], (L, D_MODEL)),
        "w_ffn_gate": nrm(ks[14], (L, D_MODEL, D_FF), D_MODEL),
        "w_ffn_up": nrm(ks[15], (L, D_MODEL, D_FF), D_MODEL),
        "w_ffn_down": nrm(ks[16], (L, D_FF, D_MODEL), D_FF),
        "norm2_post": gain(ks[17], (L, D_MODEL)),
    }


def reference(x, norm1_pre, w_in, v_ln_g, v_ln_b, sgu_w, sgu_b, pool_w, pool_scale,
              w_a_out, w_b_out, w_mix_out, norm1_post, norm2_pre, w_ffn_gate,
              w_ffn_up, w_ffn_down, norm2_post):
    s1 = D_SGU
    s2 = 2 * D_SGU
    s3 = s2 + D_POOL
    s4 = s3 + D_MODEL
    for l in range(DEPTH):
        h = rms_norm(x, norm1_pre[l])
        z = jnp.einsum('bsd,de->bse', h, w_in[l])
        uv = jax.nn.gelu(z[..., :s2], approximate=False)
        u, v = uv[..., :s1], uv[..., s1:]
        p = z[..., s2:s3]
        gate_a = jax.nn.sigmoid(z[..., s3:s4])
        gate_b = jax.nn.sigmoid(z[..., s4:])
        y_a = sgu_mixer(u, v, v_ln_g[l], v_ln_b[l], sgu_w[l], sgu_b[l])
        y_b = pool_mixer(p, pool_w[l], pool_scale[l])
        merged = (gate_a * jnp.einsum('bsc,cd->bsd', y_a, w_a_out[l])
                  + gate_b * jnp.einsum('bsc,cd->bsd', y_b, w_b_out[l]))
        mix_out = jnp.einsum('bsd,de->bse', merged, w_mix_out[l])
        x = x + rms_norm(mix_out, norm1_post[l])
        h = rms_norm(x, norm2_pre[l])
        hid = (jax.nn.silu(jnp.einsum('bsd,df->bsf', h, w_ffn_gate[l]))
               * jnp.einsum('bsd,df->bsf', h, w_ffn_up[l]))
        ffn_out = jnp.einsum('bsf,fd->bsd', hid, w_ffn_down[l])
        x = x + rms_norm(ffn_out, norm2_post[l])
    return x
```

```python
import functools

import jax
import jax.numpy as jnp
from jax import lax
from jax.experimental import pallas as pl
from jax.experimental.pallas import tpu as pltpu

F32 = jnp.float32
BF16 = jnp.bfloat16

EPS = 1e-6
CHUNK = 64
SGU_BLOCK = 128
SGU_HEADS = 8
POOL_WINDOWS = (2, 4, 8, 16)
POOL_HALO = 16

V7X_VMEM_LIMIT_BYTES = 58 * 1024 * 1024

TM_IN = 1024
TN_IN = 1024
TM_MIX = 256
TM_FFN = 1024
TF_FFN = 512


def _rms_scale(x):
    return lax.rsqrt(jnp.mean(x * x, axis=-1, keepdims=True) + EPS)


def _gelu(x):
    return 0.5 * x * (1.0 + lax.erf(x * (2.0 ** -0.5)))


def _sigmoid(x):
    return 1.0 / (1.0 + jnp.exp(-x))


def _inproj_kernel(x_ref, g_ref, w_ref, z_ref, h_ref, *, n_gelu, n_plain):
    j = pl.program_id(1)

    @pl.when(j == 0)
    def _():
        x = x_ref[...]
        h_ref[...] = (x * _rms_scale(x) * g_ref[...]).astype(BF16)

    def proj():
        return jnp.dot(h_ref[...], w_ref[...], preferred_element_type=F32)

    @pl.when(j < n_gelu)
    def _():
        z_ref[...] = _gelu(proj()).astype(BF16)

    @pl.when(jnp.logical_and(j >= n_gelu, j < n_gelu + n_plain))
    def _():
        z_ref[...] = proj().astype(BF16)

    @pl.when(j >= n_gelu + n_plain)
    def _():
        z_ref[...] = _sigmoid(proj()).astype(BF16)


def _inproj(x2, g, w_bf16, *, n_gelu, n_plain):
    t, d = x2.shape
    n_col = w_bf16.shape[1] // TN_IN
    return pl.pallas_call(
        functools.partial(_inproj_kernel, n_gelu=n_gelu, n_plain=n_plain),
        grid=(t // TM_IN, n_col),
        in_specs=[
            pl.BlockSpec((TM_IN, d), lambda i, j: (i, 0)),
            pl.BlockSpec((1, d), lambda i, j: (0, 0)),
            pl.BlockSpec((d, TN_IN), lambda i, j: (0, j)),
        ],
        out_specs=pl.BlockSpec((None, TM_IN, TN_IN), lambda i, j: (j, i, 0)),
        out_shape=jax.ShapeDtypeStruct((n_col, t, TN_IN), BF16),
        scratch_shapes=[pltpu.VMEM((TM_IN, d), BF16)],
        compiler_params=pltpu.CompilerParams(
            dimension_semantics=("arbitrary", "arbitrary"),
            vmem_limit_bytes=V7X_VMEM_LIMIT_BYTES),
        name="inproj",
    )(x2, g, w_bf16)


def _mix_kernel(u_ref, v_ref, p_ref, ga0_ref, ga1_ref, gb0_ref, gb1_ref, x_ref,
                lng_ref, lnb_ref, sguw_ref, sgubt_ref, poolw_ref, pscale_ref,
                wa_ref, wb_ref, wmix_ref, n1post_ref,
                o_ref, wm_ref, pbuf_ref, ya_ref, yb_ref, *, seq):
    i = pl.program_id(0)
    tm = u_ref.shape[0]
    d_sgu = u_ref.shape[1]
    head_dim = d_sgu // SGU_HEADS
    pool_dim = p_ref.shape[1] // len(POOL_WINDOWS)
    row0 = (i * tm) % seq

    @pl.when(i == 0)
    def _():
        r = lax.broadcasted_iota(jnp.int32, (SGU_BLOCK, SGU_BLOCK), 0) // CHUNK
        c = lax.broadcasted_iota(jnp.int32, (SGU_BLOCK, SGU_BLOCK), 1) // CHUNK
        for h in range(SGU_HEADS):
            wm_ref[h] = jnp.where(r >= c, sguw_ref[h], 0.0).astype(BF16)

    v = v_ref[...].astype(F32)
    mu = jnp.mean(v, axis=-1, keepdims=True)
    vc = v - mu
    vn = vc * lax.rsqrt(jnp.mean(vc * vc, axis=-1, keepdims=True) + EPS)
    vn = (vn * lng_ref[...] + lnb_ref[...]).astype(BF16)
    for blk in range(tm // SGU_BLOCK):
        rows = slice(blk * SGU_BLOCK, (blk + 1) * SGU_BLOCK)
        for h in range(SGU_HEADS):
            cols = slice(h * head_dim, (h + 1) * head_dim)
            mixed = jnp.dot(wm_ref[h], vn[rows, cols], preferred_element_type=F32)
            mixed = mixed + sgubt_ref[:, h:h + 1]
            ya_ref[rows, cols] = (u_ref[rows, cols].astype(F32) * mixed).astype(BF16)

    @pl.when(row0 == 0)
    def _():
        pbuf_ref[0:POOL_HALO, :] = jnp.zeros((POOL_HALO, pbuf_ref.shape[1]), F32)

    pbuf_ref[POOL_HALO:POOL_HALO + tm, :] = p_ref[...].astype(F32)
    pos = row0 + lax.broadcasted_iota(jnp.int32, (tm, 1), 0)
    for gi, w in enumerate(POOL_WINDOWS):
        cols = slice(gi * pool_dim, (gi + 1) * pool_dim)
        cur = pbuf_ref[POOL_HALO:POOL_HALO + tm, cols]
        s = cur
        for k in range(1, w):
            s = s + pbuf_ref[POOL_HALO - k:POOL_HALO - k + tm, cols]
        cnt = jnp.minimum(pos + 1, w).astype(F32)
        pooled = (s / cnt - cur).astype(BF16)
        yb = jnp.dot(pooled, poolw_ref[gi], preferred_element_type=F32)
        yb_ref[:, cols] = (yb * pscale_ref[:, cols]).astype(BF16)
    pbuf_ref[0:POOL_HALO, :] = pbuf_ref[tm:tm + POOL_HALO, :]

    a = jnp.dot(ya_ref[...], wa_ref[...], preferred_element_type=F32)
    b = jnp.dot(yb_ref[...], wb_ref[...], preferred_element_type=F32)
    half = a.shape[1] // 2
    m0 = ga0_ref[...].astype(F32) * a[:, :half] + gb0_ref[...].astype(F32) * b[:, :half]
    m1 = ga1_ref[...].astype(F32) * a[:, half:] + gb1_ref[...].astype(F32) * b[:, half:]
    merged = jnp.concatenate([m0, m1], axis=1).astype(BF16)
    mix = jnp.dot(merged, wmix_ref[...], preferred_element_type=F32)
    o_ref[...] = x_ref[...] + mix * _rms_scale(mix) * n1post_ref[...]


def _mix(z3, x2, lng, lnb, sguw, sgubt, poolw_bf16, pscale, wa, wb, wmix, n1post, *, seq):
    t, d = x2.shape
    d_sgu = z3.shape[2]
    const = lambda *shape: pl.BlockSpec(shape, lambda i: (0,) * len(shape),
                                        pipeline_mode=pl.Buffered(1))
    zspec = lambda k: pl.BlockSpec((None, TM_MIX, d_sgu), lambda i: (k, i, 0))
    return pl.pallas_call(
        functools.partial(_mix_kernel, seq=seq),
        grid=(t // TM_MIX,),
        in_specs=[zspec(k) for k in range(7)] + [
            pl.BlockSpec((TM_MIX, d), lambda i: (i, 0)),
            const(1, d_sgu), const(1, d_sgu),
            const(*sguw.shape), const(*sgubt.shape),
            const(*poolw_bf16.shape), const(1, pscale.shape[1]),
            const(*wa.shape), const(*wb.shape), const(*wmix.shape), const(1, d),
        ],
        out_specs=pl.BlockSpec((TM_MIX, d), lambda i: (i, 0)),
        out_shape=jax.ShapeDtypeStruct((t, d), F32),
        scratch_shapes=[
            pltpu.VMEM(sguw.shape, BF16),
            pltpu.VMEM((POOL_HALO + TM_MIX, d_sgu), F32),
            pltpu.VMEM((TM_MIX, d_sgu), BF16),
            pltpu.VMEM((TM_MIX, d_sgu), BF16),
        ],
        compiler_params=pltpu.CompilerParams(
            dimension_semantics=("arbitrary",),
            vmem_limit_bytes=V7X_VMEM_LIMIT_BYTES),
        name="mixers",
    )(z3, z3, z3, z3, z3, z3, z3, x2, lng, lnb, sguw, sgubt, poolw_bf16, pscale,
      wa, wb, wmix, n1post)


def _ffn_kernel(x_ref, n2pre_ref, wg_ref, wu_ref, wd_ref, n2post_ref, o_ref, h_ref):
    f = pl.program_id(1)

    @pl.when(f == 0)
    def _():
        x = x_ref[...]
        h_ref[...] = (x * _rms_scale(x) * n2pre_ref[...]).astype(BF16)
        o_ref[...] = jnp.zeros(o_ref.shape, F32)

    h = h_ref[...]
    g = jnp.dot(h, wg_ref[...], preferred_element_type=F32)
    u = jnp.dot(h, wu_ref[...], preferred_element_type=F32)
    hid = (g * _sigmoid(g) * u).astype(BF16)
    o_ref[...] += jnp.dot(hid, wd_ref[...], preferred_element_type=F32)

    @pl.when(f == pl.num_programs(1) - 1)
    def _():
        y = o_ref[...]
        o_ref[...] = x_ref[...] + y * _rms_scale(y) * n2post_ref[...]


def _ffn(x1, n2pre, wg, wu, wd, n2post):
    t, d = x1.shape
    d_ff = wg.shape[1]
    return pl.pallas_call(
        _ffn_kernel,
        grid=(t // TM_FFN, d_ff // TF_FFN),
        in_specs=[
            pl.BlockSpec((TM_FFN, d), lambda i, f: (i, 0), pipeline_mode=pl.Buffered(1)),
            pl.BlockSpec((1, d), lambda i, f: (0, 0)),
            pl.BlockSpec((d, TF_FFN), lambda i, f: (0, f)),
            pl.BlockSpec((d, TF_FFN), lambda i, f: (0, f)),
            pl.BlockSpec((TF_FFN, d), lambda i, f: (f, 0)),
            pl.BlockSpec((1, d), lambda i, f: (0, 0)),
        ],
        out_specs=pl.BlockSpec((TM_FFN, d), lambda i, f: (i, 0)),
        out_shape=jax.ShapeDtypeStruct((t, d), F32),
        scratch_shapes=[pltpu.VMEM((TM_FFN, d), BF16)],
        compiler_params=pltpu.CompilerParams(
            dimension_semantics=("arbitrary", "arbitrary"),
            vmem_limit_bytes=V7X_VMEM_LIMIT_BYTES),
        name="ffn",
    )(x1, n2pre, wg, wu, wd, n2post)


def kernel(x, norm1_pre, w_in, v_ln_g, v_ln_b, sgu_w, sgu_b, pool_w, pool_scale,
           w_a_out, w_b_out, w_mix_out, norm1_post, norm2_pre, w_ffn_gate,
           w_ffn_up, w_ffn_down, norm2_post):
    bsz, seq, d = x.shape
    depth = w_in.shape[0]
    d_sgu = v_ln_g.shape[1]
    d_pool = pool_scale.shape[1]
    assert seq % TM_IN == 0 and seq % TM_MIX == 0 and seq % TM_FFN == 0
    assert d_sgu == TN_IN and d_pool == TN_IN and d % TN_IN == 0
    assert TM_MIX % SGU_BLOCK == 0 and d_pool % len(POOL_WINDOWS) == 0

    x2 = x.reshape(bsz * seq, d)
    row = lambda a: a.reshape(1, -1)
    for l in range(depth):
        z3 = _inproj(x2, row(norm1_pre[l]), w_in[l].astype(BF16),
                     n_gelu=2 * d_sgu // TN_IN, n_plain=d_pool // TN_IN)
        x2 = _mix(z3, x2, row(v_ln_g[l]), row(v_ln_b[l]), sgu_w[l], sgu_b[l].T,
                  pool_w[l].astype(BF16), row(pool_scale[l]),
                  w_a_out[l].astype(BF16), w_b_out[l].astype(BF16),
                  w_mix_out[l].astype(BF16), row(norm1_post[l]), seq=seq)
        x2 = _ffn(x2, row(norm2_pre[l]), w_ffn_gate[l].astype(BF16),
                  w_ffn_up[l].astype(BF16), w_ffn_down[l].astype(BF16),
                  row(norm2_post[l]))
    return x2.reshape(bsz, seq, d)
```

```python
import functools

import jax
import jax.numpy as jnp
from jax import lax
from jax.experimental import pallas as pl
from jax.experimental.pallas import tpu as pltpu

F32 = jnp.float32
BF16 = jnp.bfloat16

EPS = 1e-6
CHUNK = 64
SGU_BLOCK = 128
SGU_HEADS = 8
POOL_WINDOWS = (2, 4, 8, 16)
POOL_HALO = 16

V7X_VMEM_LIMIT_BYTES = 58 * 1024 * 1024

TM_IN = 1024
TN_IN = 1024
TM_MIX = 256
TM_FFN = 1024
TF_FFN = 512
MC_FFN = 256


def _rms_scale(x):
    return lax.rsqrt(jnp.mean(x * x, axis=-1, keepdims=True) + EPS)


def _gelu(x):
    return 0.5 * x * (1.0 + lax.erf(x * (2.0 ** -0.5)))


def _sigmoid(x):
    return 1.0 / (1.0 + jnp.exp(-x))


def _inproj_kernel(x_ref, g_ref, w_ref, z_ref, h_ref, *, n_gelu, n_plain):
    j = pl.program_id(1)

    @pl.when(j == 0)
    def _():
        x = x_ref[...]
        h_ref[...] = (x * _rms_scale(x) * g_ref[...]).astype(BF16)

    def proj():
        return jnp.dot(h_ref[...], w_ref[...], preferred_element_type=F32)

    @pl.when(j < n_gelu)
    def _():
        z_ref[...] = _gelu(proj()).astype(BF16)

    @pl.when(jnp.logical_and(j >= n_gelu, j < n_gelu + n_plain))
    def _():
        z_ref[...] = proj().astype(BF16)

    @pl.when(j >= n_gelu + n_plain)
    def _():
        z_ref[...] = _sigmoid(proj()).astype(BF16)


def _inproj(x2, g, w_bf16, *, n_gelu, n_plain):
    t, d = x2.shape
    n_col = w_bf16.shape[1] // TN_IN
    return pl.pallas_call(
        functools.partial(_inproj_kernel, n_gelu=n_gelu, n_plain=n_plain),
        grid=(t // TM_IN, n_col),
        in_specs=[
            pl.BlockSpec((TM_IN, d), lambda i, j: (i, 0)),
            pl.BlockSpec((1, d), lambda i, j: (0, 0)),
            pl.BlockSpec((d, TN_IN), lambda i, j: (0, j)),
        ],
        out_specs=pl.BlockSpec((None, TM_IN, TN_IN), lambda i, j: (j, i, 0)),
        out_shape=jax.ShapeDtypeStruct((n_col, t, TN_IN), BF16),
        scratch_shapes=[pltpu.VMEM((TM_IN, d), BF16)],
        compiler_params=pltpu.CompilerParams(
            dimension_semantics=("arbitrary", "arbitrary"),
            vmem_limit_bytes=V7X_VMEM_LIMIT_BYTES),
        name="inproj",
    )(x2, g, w_bf16)


def _mix_kernel(u_ref, v_ref, p_ref, ga0_ref, ga1_ref, gb0_ref, gb1_ref, x_ref,
                lng_ref, lnb_ref, sguw_ref, sgubt_ref, poolw_ref, pscale_ref,
                wa_ref, wb_ref, wmix_ref, n1post_ref,
                o_ref, wm_ref, pbuf_ref, ya_ref, yb_ref, *, seq):
    i = pl.program_id(0)
    tm = u_ref.shape[0]
    d_sgu = u_ref.shape[1]
    head_dim = d_sgu // SGU_HEADS
    pool_dim = p_ref.shape[1] // len(POOL_WINDOWS)
    row0 = (i * tm) % seq

    @pl.when(i == 0)
    def _():
        r = lax.broadcasted_iota(jnp.int32, (SGU_BLOCK, SGU_BLOCK), 0) // CHUNK
        c = lax.broadcasted_iota(jnp.int32, (SGU_BLOCK, SGU_BLOCK), 1) // CHUNK
        for h in range(SGU_HEADS):
            wm_ref[h] = jnp.where(r >= c, sguw_ref[h], 0.0).astype(BF16)

    v = v_ref[...].astype(F32)
    mu = jnp.mean(v, axis=-1, keepdims=True)
    vc = v - mu
    vn = vc * lax.rsqrt(jnp.mean(vc * vc, axis=-1, keepdims=True) + EPS)
    vn = (vn * lng_ref[...] + lnb_ref[...]).astype(BF16)
    for blk in range(tm // SGU_BLOCK):
        rows = slice(blk * SGU_BLOCK, (blk + 1) * SGU_BLOCK)
        for h in range(SGU_HEADS):
            cols = slice(h * head_dim, (h + 1) * head_dim)
            mixed = jnp.dot(wm_ref[h], vn[rows, cols], preferred_element_type=F32)
            mixed = mixed + sgubt_ref[:, h:h + 1]
            ya_ref[rows, cols] = (u_ref[rows, cols].astype(F32) * mixed).astype(BF16)

    @pl.when(row0 == 0)
    def _():
        pbuf_ref[0:POOL_HALO, :] = jnp.zeros((POOL_HALO, pbuf_ref.shape[1]), F32)

    pbuf_ref[POOL_HALO:POOL_HALO + tm, :] = p_ref[...].astype(F32)
    pos = row0 + lax.broadcasted_iota(jnp.int32, (tm, 1), 0)
    for gi, w in enumerate(POOL_WINDOWS):
        cols = slice(gi * pool_dim, (gi + 1) * pool_dim)
        cur = pbuf_ref[POOL_HALO:POOL_HALO + tm, cols]
        s = cur
        for k in range(1, w):
            s = s + pbuf_ref[POOL_HALO - k:POOL_HALO - k + tm, cols]
        cnt = jnp.minimum(pos + 1, w).astype(F32)
        pooled = (s / cnt - cur).astype(BF16)
        yb = jnp.dot(pooled, poolw_ref[gi], preferred_element_type=F32)
        yb_ref[:, cols] = (yb * pscale_ref[:, cols]).astype(BF16)
    pbuf_ref[0:POOL_HALO, :] = pbuf_ref[tm:tm + POOL_HALO, :]

    a = jnp.dot(ya_ref[...], wa_ref[...], preferred_element_type=F32)
    b = jnp.dot(yb_ref[...], wb_ref[...], preferred_element_type=F32)
    half = a.shape[1] // 2
    m0 = ga0_ref[...].astype(F32) * a[:, :half] + gb0_ref[...].astype(F32) * b[:, :half]
    m1 = ga1_ref[...].astype(F32) * a[:, half:] + gb1_ref[...].astype(F32) * b[:, half:]
    merged = jnp.concatenate([m0, m1], axis=1).astype(BF16)
    mix = jnp.dot(merged, wmix_ref[...], preferred_element_type=F32)
    o_ref[...] = x_ref[...] + mix * _rms_scale(mix) * n1post_ref[...]


def _mix(z3, x2, lng, lnb, sguw, sgubt, poolw_bf16, pscale, wa, wb, wmix, n1post, *, seq):
    t, d = x2.shape
    d_sgu = z3.shape[2]
    const = lambda *shape: pl.BlockSpec(shape, lambda i: (0,) * len(shape),
                                        pipeline_mode=pl.Buffered(1))
    zspec = lambda k: pl.BlockSpec((None, TM_MIX, d_sgu), lambda i: (k, i, 0))
    return pl.pallas_call(
        functools.partial(_mix_kernel, seq=seq),
        grid=(t // TM_MIX,),
        in_specs=[zspec(k) for k in range(7)] + [
            pl.BlockSpec((TM_MIX, d), lambda i: (i, 0)),
            const(1, d_sgu), const(1, d_sgu),
            const(*sguw.shape), const(*sgubt.shape),
            const(*poolw_bf16.shape), const(1, pscale.shape[1]),
            const(*wa.shape), const(*wb.shape), const(*wmix.shape), const(1, d),
        ],
        out_specs=pl.BlockSpec((TM_MIX, d), lambda i: (i, 0)),
        out_shape=jax.ShapeDtypeStruct((t, d), F32),
        scratch_shapes=[
            pltpu.VMEM(sguw.shape, BF16),
            pltpu.VMEM((POOL_HALO + TM_MIX, d_sgu), F32),
            pltpu.VMEM((TM_MIX, d_sgu), BF16),
            pltpu.VMEM((TM_MIX, d_sgu), BF16),
        ],
        compiler_params=pltpu.CompilerParams(
            dimension_semantics=("arbitrary",),
            vmem_limit_bytes=V7X_VMEM_LIMIT_BYTES),
        name="mixers",
    )(z3, z3, z3, z3, z3, z3, z3, x2, lng, lnb, sguw, sgubt, poolw_bf16, pscale,
      wa, wb, wmix, n1post)


def _ffn_kernel(x_ref, n2pre_ref, wg_ref, wu_ref, wd_ref, n2post_ref, o_ref,
                h_ref, wgb_ref, wub_ref, wdb_ref):
    f = pl.program_id(1)
    nf = pl.num_programs(1)
    tm = x_ref.shape[0]

    def step(first, last):
        wgb_ref[...] = wg_ref[...].astype(BF16)
        wub_ref[...] = wu_ref[...].astype(BF16)
        wdb_ref[...] = wd_ref[...].astype(BF16)
        for c in range(tm // MC_FFN):
            rows = pl.ds(c * MC_FFN, MC_FFN)
            if first:
                x = x_ref[rows, :]
                h = (x * _rms_scale(x) * n2pre_ref[...]).astype(BF16)
                h_ref[rows, :] = h
            else:
                h = h_ref[rows, :]
            g = jnp.dot(h, wgb_ref[...], preferred_element_type=F32)
            u = jnp.dot(h, wub_ref[...], preferred_element_type=F32)
            hid = (g * _sigmoid(g) * u).astype(BF16)
            y = jnp.dot(hid, wdb_ref[...], preferred_element_type=F32)
            if not first:
                y = o_ref[rows, :] + y
            if last:
                y = x_ref[rows, :] + y * _rms_scale(y) * n2post_ref[...]
            o_ref[rows, :] = y

    pl.when(f == 0)(functools.partial(step, True, False))
    pl.when(jnp.logical_and(f > 0, f < nf - 1))(functools.partial(step, False, False))
    pl.when(f == nf - 1)(functools.partial(step, False, True))


def _ffn(x1, n2pre, wg, wu, wd, n2post):
    t, d = x1.shape
    d_ff = wg.shape[1]
    assert d_ff // TF_FFN >= 2
    return pl.pallas_call(
        _ffn_kernel,
        grid=(t // TM_FFN, d_ff // TF_FFN),
        in_specs=[
            pl.BlockSpec((TM_FFN, d), lambda i, f: (i, 0), pipeline_mode=pl.Buffered(1)),
            pl.BlockSpec((1, d), lambda i, f: (0, 0)),
            pl.BlockSpec((d, TF_FFN), lambda i, f: (0, f)),
            pl.BlockSpec((d, TF_FFN), lambda i, f: (0, f)),
            pl.BlockSpec((TF_FFN, d), lambda i, f: (f, 0)),
            pl.BlockSpec((1, d), lambda i, f: (0, 0)),
        ],
        out_specs=pl.BlockSpec((TM_FFN, d), lambda i, f: (i, 0), pipeline_mode=pl.Buffered(1)),
        out_shape=jax.ShapeDtypeStruct((t, d), F32),
        scratch_shapes=[
            pltpu.VMEM((TM_FFN, d), BF16),
            pltpu.VMEM((d, TF_FFN), BF16),
            pltpu.VMEM((d, TF_FFN), BF16),
            pltpu.VMEM((TF_FFN, d), BF16),
        ],
        compiler_params=pltpu.CompilerParams(
            dimension_semantics=("arbitrary", "arbitrary"),
            vmem_limit_bytes=V7X_VMEM_LIMIT_BYTES),
        name="ffn",
    )(x1, n2pre, wg, wu, wd, n2post)


def kernel(x, norm1_pre, w_in, v_ln_g, v_ln_b, sgu_w, sgu_b, pool_w, pool_scale,
           w_a_out, w_b_out, w_mix_out, norm1_post, norm2_pre, w_ffn_gate,
           w_ffn_up, w_ffn_down, norm2_post):
    bsz, seq, d = x.shape
    depth = w_in.shape[0]
    d_sgu = v_ln_g.shape[1]
    d_pool = pool_scale.shape[1]
    assert seq % TM_IN == 0 and seq % TM_MIX == 0 and seq % TM_FFN == 0
    assert d_sgu == TN_IN and d_pool == TN_IN and d % TN_IN == 0
    assert TM_MIX % SGU_BLOCK == 0 and d_pool % len(POOL_WINDOWS) == 0

    x2 = x.reshape(bsz * seq, d)
    row = lambda a: a.reshape(1, -1)
    for l in range(depth):
        z3 = _inproj(x2, row(norm1_pre[l]), w_in[l].astype(BF16),
                     n_gelu=2 * d_sgu // TN_IN, n_plain=d_pool // TN_IN)
        x2 = _mix(z3, x2, row(v_ln_g[l]), row(v_ln_b[l]), sgu_w[l], sgu_b[l].T,
                  pool_w[l].astype(BF16), row(pool_scale[l]),
                  w_a_out[l].astype(BF16), w_b_out[l].astype(BF16),
                  w_mix_out[l].astype(BF16), row(norm1_post[l]), seq=seq)
        x2 = _ffn(x2, row(norm2_pre[l]), w_ffn_gate[l], w_ffn_up[l], w_ffn_down[l],
                  row(norm2_post[l]))
    return x2.reshape(bsz, seq, d)
```

```python
import functools

import jax
import jax.numpy as jnp
from jax import lax
from jax.experimental import pallas as pl
from jax.experimental.pallas import tpu as pltpu

F32 = jnp.float32
BF16 = jnp.bfloat16

EPS = 1e-6
CHUNK = 64
SGU_BLOCK = 128
SGU_HEADS = 8
POOL_WINDOWS = (2, 4, 8, 16)
POOL_HALO = 16

V7X_VMEM_LIMIT_BYTES = 58 * 1024 * 1024

TM_IN = 1024
TN_IN = 1024
TM_MIX = 256
TM_FFN = 1024
TF_FFN = 512
MC_FFN = 256
BF16_SUBLANES = 16


def _rms_scale(x):
    return lax.rsqrt(jnp.mean(x * x, axis=-1, keepdims=True) + EPS)


def _gelu(x):
    return 0.5 * x * (1.0 + lax.erf(x * (2.0 ** -0.5)))


def _sigmoid(x):
    return 1.0 / (1.0 + jnp.exp(-x))


def _inproj_kernel(x_ref, g_ref, w_ref, z_ref, h_ref, *, n_gelu, n_plain):
    j = pl.program_id(1)

    @pl.when(j == 0)
    def _():
        x = x_ref[...]
        h_ref[...] = (x * _rms_scale(x) * g_ref[...]).astype(BF16)

    def proj():
        return jnp.dot(h_ref[...], w_ref[...], preferred_element_type=F32)

    @pl.when(j < n_gelu)
    def _():
        z_ref[...] = _gelu(proj()).astype(BF16)

    @pl.when(jnp.logical_and(j >= n_gelu, j < n_gelu + n_plain))
    def _():
        z_ref[...] = proj().astype(BF16)

    @pl.when(j >= n_gelu + n_plain)
    def _():
        z_ref[...] = _sigmoid(proj()).astype(BF16)


def _inproj(x2, g, w_bf16, *, n_gelu, n_plain):
    t, d = x2.shape
    n_col = w_bf16.shape[1] // TN_IN
    return pl.pallas_call(
        functools.partial(_inproj_kernel, n_gelu=n_gelu, n_plain=n_plain),
        grid=(t // TM_IN, n_col),
        in_specs=[
            pl.BlockSpec((TM_IN, d), lambda i, j: (i, 0)),
            pl.BlockSpec((1, d), lambda i, j: (0, 0)),
            pl.BlockSpec((d, TN_IN), lambda i, j: (0, j)),
        ],
        out_specs=pl.BlockSpec((None, TM_IN, TN_IN), lambda i, j: (j, i, 0)),
        out_shape=jax.ShapeDtypeStruct((n_col, t, TN_IN), BF16),
        scratch_shapes=[pltpu.VMEM((TM_IN, d), BF16)],
        compiler_params=pltpu.CompilerParams(
            dimension_semantics=("arbitrary", "arbitrary"),
            vmem_limit_bytes=V7X_VMEM_LIMIT_BYTES),
        name="inproj",
    )(x2, g, w_bf16)


def _mix_kernel(u_ref, v_ref, p_ref, ga0_ref, ga1_ref, gb0_ref, gb1_ref, x_ref,
                lng_ref, lnb_ref, sguw_ref, sgubt_ref, poolw_ref, pscale_ref,
                wa_ref, wb_ref, wmix_ref, n1post_ref, fg_ref, fu_ref, fd_ref,
                o_ref, fgb_ref, fub_ref, fdb_ref, wm_ref, pbuf_ref, ya_ref, yb_ref, *, seq):
    i = pl.program_id(0)
    tm = u_ref.shape[0]
    d_sgu = u_ref.shape[1]
    head_dim = d_sgu // SGU_HEADS
    pool_dim = p_ref.shape[1] // len(POOL_WINDOWS)
    row0 = (i * tm) % seq

    @pl.when(i == 0)
    def _():
        r = lax.broadcasted_iota(jnp.int32, (SGU_BLOCK, SGU_BLOCK), 0) // CHUNK
        c = lax.broadcasted_iota(jnp.int32, (SGU_BLOCK, SGU_BLOCK), 1) // CHUNK
        for h in range(SGU_HEADS):
            wm_ref[h] = jnp.where(r >= c, sguw_ref[h], 0.0).astype(BF16)

    v = v_ref[...].astype(F32)
    mu = jnp.mean(v, axis=-1, keepdims=True)
    vc = v - mu
    vn = vc * lax.rsqrt(jnp.mean(vc * vc, axis=-1, keepdims=True) + EPS)
    vn = (vn * lng_ref[...] + lnb_ref[...]).astype(BF16)
    for blk in range(tm // SGU_BLOCK):
        rows = slice(blk * SGU_BLOCK, (blk + 1) * SGU_BLOCK)
        for h in range(SGU_HEADS):
            cols = slice(h * head_dim, (h + 1) * head_dim)
            mixed = jnp.dot(wm_ref[h], vn[rows, cols], preferred_element_type=F32)
            mixed = mixed + sgubt_ref[:, h:h + 1]
            ya_ref[rows, cols] = (u_ref[rows, cols].astype(F32) * mixed).astype(BF16)

    @pl.when(row0 == 0)
    def _():
        pbuf_ref[0:POOL_HALO, :] = jnp.zeros((POOL_HALO, pbuf_ref.shape[1]), F32)

    pbuf_ref[POOL_HALO:POOL_HALO + tm, :] = p_ref[...].astype(F32)
    pos = row0 + lax.broadcasted_iota(jnp.int32, (tm, 1), 0)
    for gi, w in enumerate(POOL_WINDOWS):
        cols = slice(gi * pool_dim, (gi + 1) * pool_dim)
        cur = pbuf_ref[POOL_HALO:POOL_HALO + tm, cols]
        s = cur
        for k in range(1, w):
            s = s + pbuf_ref[POOL_HALO - k:POOL_HALO - k + tm, cols]
        cnt = jnp.minimum(pos + 1, w).astype(F32)
        pooled = (s / cnt - cur).astype(BF16)
        yb = jnp.dot(pooled, poolw_ref[gi], preferred_element_type=F32)
        yb_ref[:, cols] = (yb * pscale_ref[:, cols]).astype(BF16)
    pbuf_ref[0:POOL_HALO, :] = pbuf_ref[tm:tm + POOL_HALO, :]

    a = jnp.dot(ya_ref[...], wa_ref[...], preferred_element_type=F32)
    b = jnp.dot(yb_ref[...], wb_ref[...], preferred_element_type=F32)
    half = a.shape[1] // 2
    m0 = ga0_ref[...].astype(F32) * a[:, :half] + gb0_ref[...].astype(F32) * b[:, :half]
    m1 = ga1_ref[...].astype(F32) * a[:, half:] + gb1_ref[...].astype(F32) * b[:, half:]
    merged = jnp.concatenate([m0, m1], axis=1).astype(BF16)
    mix = jnp.dot(merged, wmix_ref[...], preferred_element_type=F32)
    o_ref[...] = x_ref[...] + mix * _rms_scale(mix) * n1post_ref[...]

    fgb_ref[...] = fg_ref[...].astype(BF16)
    fub_ref[...] = fu_ref[...].astype(BF16)
    fdb_ref[...] = fd_ref[...].astype(BF16)


def _mix(z3, x2, lng, lnb, sguw, sgubt, poolw_bf16, pscale, wa, wb, wmix, n1post,
         ffn_ws, *, seq):
    t, d = x2.shape
    d_sgu = z3.shape[2]
    n_steps = t // TM_MIX
    assert all(w.shape[0] % (n_steps * BF16_SUBLANES) == 0 for w in ffn_ws)
    slab_specs = [pl.BlockSpec((w.shape[0] // n_steps, w.shape[1]), lambda i: (i, 0))
                  for w in ffn_ws]
    const = lambda *shape: pl.BlockSpec(shape, lambda i: (0,) * len(shape),
                                        pipeline_mode=pl.Buffered(1))
    zspec = lambda k: pl.BlockSpec((None, TM_MIX, d_sgu), lambda i: (k, i, 0))
    outs = pl.pallas_call(
        functools.partial(_mix_kernel, seq=seq),
        grid=(n_steps,),
        in_specs=[zspec(k) for k in range(7)] + [
            pl.BlockSpec((TM_MIX, d), lambda i: (i, 0)),
            const(1, d_sgu), const(1, d_sgu),
            const(*sguw.shape), const(*sgubt.shape),
            const(*poolw_bf16.shape), const(1, pscale.shape[1]),
            const(*wa.shape), const(*wb.shape), const(*wmix.shape), const(1, d),
        ] + slab_specs,
        out_specs=[pl.BlockSpec((TM_MIX, d), lambda i: (i, 0))] + slab_specs,
        out_shape=[jax.ShapeDtypeStruct((t, d), F32)]
                  + [jax.ShapeDtypeStruct(w.shape, BF16) for w in ffn_ws],
        scratch_shapes=[
            pltpu.VMEM(sguw.shape, BF16),
            pltpu.VMEM((POOL_HALO + TM_MIX, d_sgu), F32),
            pltpu.VMEM((TM_MIX, d_sgu), BF16),
            pltpu.VMEM((TM_MIX, d_sgu), BF16),
        ],
        compiler_params=pltpu.CompilerParams(
            dimension_semantics=("arbitrary",),
            vmem_limit_bytes=V7X_VMEM_LIMIT_BYTES),
        name="mixers",
    )(z3, z3, z3, z3, z3, z3, z3, x2, lng, lnb, sguw, sgubt, poolw_bf16, pscale,
      wa, wb, wmix, n1post, *ffn_ws)
    return outs[0], outs[1:]


def _ffn_kernel(x_ref, n2pre_ref, wgb_ref, wub_ref, wdb_ref, n2post_ref, o_ref, h_ref):
    f = pl.program_id(1)
    nf = pl.num_programs(1)
    tm = x_ref.shape[0]

    def step(first, last):
        for c in range(tm // MC_FFN):
            rows = pl.ds(c * MC_FFN, MC_FFN)
            if first:
                x = x_ref[rows, :]
                h = (x * _rms_scale(x) * n2pre_ref[...]).astype(BF16)
                h_ref[rows, :] = h
            else:
                h = h_ref[rows, :]
            g = jnp.dot(h, wgb_ref[...], preferred_element_type=F32)
            u = jnp.dot(h, wub_ref[...], preferred_element_type=F32)
            hid = (g * _sigmoid(g) * u).astype(BF16)
            y = jnp.dot(hid, wdb_ref[...], preferred_element_type=F32)
            if not first:
                y = o_ref[rows, :] + y
            if last:
                y = x_ref[rows, :] + y * _rms_scale(y) * n2post_ref[...]
            o_ref[rows, :] = y

    pl.when(f == 0)(functools.partial(step, True, False))
    pl.when(jnp.logical_and(f > 0, f < nf - 1))(functools.partial(step, False, False))
    pl.when(f == nf - 1)(functools.partial(step, False, True))


def _ffn(x1, n2pre, wg, wu, wd, n2post):
    t, d = x1.shape
    d_ff = wg.shape[1]
    assert d_ff // TF_FFN >= 2
    return pl.pallas_call(
        _ffn_kernel,
        grid=(t // TM_FFN, d_ff // TF_FFN),
        in_specs=[
            pl.BlockSpec((TM_FFN, d), lambda i, f: (i, 0), pipeline_mode=pl.Buffered(1)),
            pl.BlockSpec((1, d), lambda i, f: (0, 0)),
            pl.BlockSpec((d, TF_FFN), lambda i, f: (0, f)),
            pl.BlockSpec((d, TF_FFN), lambda i, f: (0, f)),
            pl.BlockSpec((TF_FFN, d), lambda i, f: (f, 0)),
            pl.BlockSpec((1, d), lambda i, f: (0, 0)),
        ],
        out_specs=pl.BlockSpec((TM_FFN, d), lambda i, f: (i, 0)),
        out_shape=jax.ShapeDtypeStruct((t, d), F32),
        scratch_shapes=[pltpu.VMEM((TM_FFN, d), BF16)],
        compiler_params=pltpu.CompilerParams(
            dimension_semantics=("arbitrary", "arbitrary"),
            vmem_limit_bytes=V7X_VMEM_LIMIT_BYTES),
        name="ffn",
    )(x1, n2pre, wg, wu, wd, n2post)


def kernel(x, norm1_pre, w_in, v_ln_g, v_ln_b, sgu_w, sgu_b, pool_w, pool_scale,
           w_a_out, w_b_out, w_mix_out, norm1_post, norm2_pre, w_ffn_gate,
           w_ffn_up, w_ffn_down, norm2_post):
    bsz, seq, d = x.shape
    depth = w_in.shape[0]
    d_sgu = v_ln_g.shape[1]
    d_pool = pool_scale.shape[1]
    assert seq % TM_IN == 0 and seq % TM_MIX == 0 and seq % TM_FFN == 0
    assert d_sgu == TN_IN and d_pool == TN_IN and d % TN_IN == 0
    assert TM_MIX % SGU_BLOCK == 0 and d_pool % len(POOL_WINDOWS) == 0

    x2 = x.reshape(bsz * seq, d)
    row = lambda a: a.reshape(1, -1)
    for l in range(depth):
        z3 = _inproj(x2, row(norm1_pre[l]), w_in[l].astype(BF16),
                     n_gelu=2 * d_sgu // TN_IN, n_plain=d_pool // TN_IN)
        x2, (wg, wu, wd) = _mix(
            z3, x2, row(v_ln_g[l]), row(v_ln_b[l]), sgu_w[l], sgu_b[l].T,
            pool_w[l].astype(BF16), row(pool_scale[l]),
            w_a_out[l].astype(BF16), w_b_out[l].astype(BF16),
            w_mix_out[l].astype(BF16), row(norm1_post[l]),
            [w_ffn_gate[l], w_ffn_up[l], w_ffn_down[l]], seq=seq)
        x2 = _ffn(x2, row(norm2_pre[l]), wg, wu, wd, row(norm2_post[l]))
    return x2.reshape(bsz, seq, d)
```

```python
import functools

import jax
import jax.numpy as jnp
from jax import lax
from jax.experimental import pallas as pl
from jax.experimental.pallas import tpu as pltpu

F32 = jnp.float32
BF16 = jnp.bfloat16

EPS = 1e-6
CHUNK = 64
SGU_BLOCK = 128
SGU_HEADS = 8
POOL_WINDOWS = (2, 4, 8, 16)
POOL_HALO = 16

V7X_VMEM_LIMIT_BYTES = 58 * 1024 * 1024

TM_IN = 1024
TN_IN = 1024
TM_MIX = 256
TM_FFN = 1024
TF_FFN = 512
NORM_ROWS = 256
BF16_SUBLANES = 16


def _rms_scale(x):
    return lax.rsqrt(jnp.mean(x * x, axis=-1, keepdims=True) + EPS)


def _gelu(x):
    return 0.5 * x * (1.0 + lax.erf(x * (2.0 ** -0.5)))


def _sigmoid(x):
    return 1.0 / (1.0 + jnp.exp(-x))


def _inproj_kernel(x_ref, g_ref, w_ref, z_ref, h_ref, *, n_gelu, n_plain):
    j = pl.program_id(1)

    @pl.when(j == 0)
    def _():
        x = x_ref[...]
        h_ref[...] = (x * _rms_scale(x) * g_ref[...]).astype(BF16)

    def proj():
        return jnp.dot(h_ref[...], w_ref[...], preferred_element_type=F32)

    @pl.when(j < n_gelu)
    def _():
        z_ref[...] = _gelu(proj()).astype(BF16)

    @pl.when(jnp.logical_and(j >= n_gelu, j < n_gelu + n_plain))
    def _():
        z_ref[...] = proj().astype(BF16)

    @pl.when(j >= n_gelu + n_plain)
    def _():
        z_ref[...] = _sigmoid(proj()).astype(BF16)


def _inproj(x2, g, w_bf16, *, n_gelu, n_plain):
    t, d = x2.shape
    n_col = w_bf16.shape[1] // TN_IN
    return pl.pallas_call(
        functools.partial(_inproj_kernel, n_gelu=n_gelu, n_plain=n_plain),
        grid=(t // TM_IN, n_col),
        in_specs=[
            pl.BlockSpec((TM_IN, d), lambda i, j: (i, 0)),
            pl.BlockSpec((1, d), lambda i, j: (0, 0)),
            pl.BlockSpec((d, TN_IN), lambda i, j: (0, j)),
        ],
        out_specs=pl.BlockSpec((None, TM_IN, TN_IN), lambda i, j: (j, i, 0)),
        out_shape=jax.ShapeDtypeStruct((n_col, t, TN_IN), BF16),
        scratch_shapes=[pltpu.VMEM((TM_IN, d), BF16)],
        compiler_params=pltpu.CompilerParams(
            dimension_semantics=("arbitrary", "arbitrary"),
            vmem_limit_bytes=V7X_VMEM_LIMIT_BYTES),
        name="inproj",
    )(x2, g, w_bf16)


def _mix_kernel(u_ref, v_ref, p_ref, ga0_ref, ga1_ref, gb0_ref, gb1_ref, x_ref,
                lng_ref, lnb_ref, sguw_ref, sgubt_ref, poolw_ref, pscale_ref,
                wa_ref, wb_ref, wmix_ref, n1post_ref, fg_ref, fu_ref, fd_ref,
                o_ref, fgb_ref, fub_ref, fdb_ref, wm_ref, pbuf_ref, ya_ref, yb_ref, *, seq):
    i = pl.program_id(0)
    tm = u_ref.shape[0]
    d_sgu = u_ref.shape[1]
    head_dim = d_sgu // SGU_HEADS
    pool_dim = p_ref.shape[1] // len(POOL_WINDOWS)
    row0 = (i * tm) % seq

    @pl.when(i == 0)
    def _():
        r = lax.broadcasted_iota(jnp.int32, (SGU_BLOCK, SGU_BLOCK), 0) // CHUNK
        c = lax.broadcasted_iota(jnp.int32, (SGU_BLOCK, SGU_BLOCK), 1) // CHUNK
        for h in range(SGU_HEADS):
            wm_ref[h] = jnp.where(r >= c, sguw_ref[h], 0.0).astype(BF16)

    v = v_ref[...].astype(F32)
    mu = jnp.mean(v, axis=-1, keepdims=True)
    vc = v - mu
    vn = vc * lax.rsqrt(jnp.mean(vc * vc, axis=-1, keepdims=True) + EPS)
    vn = (vn * lng_ref[...] + lnb_ref[...]).astype(BF16)
    for blk in range(tm // SGU_BLOCK):
        rows = slice(blk * SGU_BLOCK, (blk + 1) * SGU_BLOCK)
        for h in range(SGU_HEADS):
            cols = slice(h * head_dim, (h + 1) * head_dim)
            mixed = jnp.dot(wm_ref[h], vn[rows, cols], preferred_element_type=F32)
            mixed = mixed + sgubt_ref[:, h:h + 1]
            ya_ref[rows, cols] = (u_ref[rows, cols].astype(F32) * mixed).astype(BF16)

    @pl.when(row0 == 0)
    def _():
        pbuf_ref[0:POOL_HALO, :] = jnp.zeros((POOL_HALO, pbuf_ref.shape[1]), F32)

    pbuf_ref[POOL_HALO:POOL_HALO + tm, :] = p_ref[...].astype(F32)
    pos = row0 + lax.broadcasted_iota(jnp.int32, (tm, 1), 0)
    for gi, w in enumerate(POOL_WINDOWS):
        cols = slice(gi * pool_dim, (gi + 1) * pool_dim)
        cur = pbuf_ref[POOL_HALO:POOL_HALO + tm, cols]
        s = cur
        for k in range(1, w):
            s = s + pbuf_ref[POOL_HALO - k:POOL_HALO - k + tm, cols]
        cnt = jnp.minimum(pos + 1, w).astype(F32)
        pooled = (s / cnt - cur).astype(BF16)
        yb = jnp.dot(pooled, poolw_ref[gi], preferred_element_type=F32)
        yb_ref[:, cols] = (yb * pscale_ref[:, cols]).astype(BF16)
    pbuf_ref[0:POOL_HALO, :] = pbuf_ref[tm:tm + POOL_HALO, :]

    a = jnp.dot(ya_ref[...], wa_ref[...], preferred_element_type=F32)
    b = jnp.dot(yb_ref[...], wb_ref[...], preferred_element_type=F32)
    half = a.shape[1] // 2
    m0 = ga0_ref[...].astype(F32) * a[:, :half] + gb0_ref[...].astype(F32) * b[:, :half]
    m1 = ga1_ref[...].astype(F32) * a[:, half:] + gb1_ref[...].astype(F32) * b[:, half:]
    merged = jnp.concatenate([m0, m1], axis=1).astype(BF16)
    mix = jnp.dot(merged, wmix_ref[...], preferred_element_type=F32)
    o_ref[...] = x_ref[...] + mix * _rms_scale(mix) * n1post_ref[...]

    fgb_ref[...] = fg_ref[...].astype(BF16)
    fub_ref[...] = fu_ref[...].astype(BF16)
    fdb_ref[...] = fd_ref[...].astype(BF16)


def _mix(z3, x2, lng, lnb, sguw, sgubt, poolw_bf16, pscale, wa, wb, wmix, n1post,
         ffn_ws, *, seq):
    t, d = x2.shape
    d_sgu = z3.shape[2]
    n_steps = t // TM_MIX
    assert all(w.shape[0] % (n_steps * BF16_SUBLANES) == 0 for w in ffn_ws)
    slab_specs = [pl.BlockSpec((w.shape[0] // n_steps, w.shape[1]), lambda i: (i, 0))
                  for w in ffn_ws]
    const = lambda *shape: pl.BlockSpec(shape, lambda i: (0,) * len(shape),
                                        pipeline_mode=pl.Buffered(1))
    zspec = lambda k: pl.BlockSpec((None, TM_MIX, d_sgu), lambda i: (k, i, 0))
    outs = pl.pallas_call(
        functools.partial(_mix_kernel, seq=seq),
        grid=(n_steps,),
        in_specs=[zspec(k) for k in range(7)] + [
            pl.BlockSpec((TM_MIX, d), lambda i: (i, 0)),
            const(1, d_sgu), const(1, d_sgu),
            const(*sguw.shape), const(*sgubt.shape),
            const(*poolw_bf16.shape), const(1, pscale.shape[1]),
            const(*wa.shape), const(*wb.shape), const(*wmix.shape), const(1, d),
        ] + slab_specs,
        out_specs=[pl.BlockSpec((TM_MIX, d), lambda i: (i, 0))] + slab_specs,
        out_shape=[jax.ShapeDtypeStruct((t, d), F32)]
                  + [jax.ShapeDtypeStruct(w.shape, BF16) for w in ffn_ws],
        scratch_shapes=[
            pltpu.VMEM(sguw.shape, BF16),
            pltpu.VMEM((POOL_HALO + TM_MIX, d_sgu), F32),
            pltpu.VMEM((TM_MIX, d_sgu), BF16),
            pltpu.VMEM((TM_MIX, d_sgu), BF16),
        ],
        compiler_params=pltpu.CompilerParams(
            dimension_semantics=("arbitrary",),
            vmem_limit_bytes=V7X_VMEM_LIMIT_BYTES),
        name="mixers",
    )(z3, z3, z3, z3, z3, z3, z3, x2, lng, lnb, sguw, sgubt, poolw_bf16, pscale,
      wa, wb, wmix, n1post, *ffn_ws)
    return outs[0], outs[1:]


def _ffn_kernel(x_ref, n2pre_ref, wg_hbm, wu_hbm, wd_hbm, n2post_ref, o_ref,
                h_ref, wg_buf, wu_buf, wd_buf, sem):
    i = pl.program_id(0)
    n_tiles = pl.num_programs(0)
    tf = wg_buf.shape[2]
    nf = wg_hbm.shape[1] // tf

    def weight_copies(f, slot):
        cols = pl.ds(pl.multiple_of(f * tf, tf), tf)
        return (
            pltpu.make_async_copy(wg_hbm.at[:, cols], wg_buf.at[slot], sem.at[0, slot]),
            pltpu.make_async_copy(wu_hbm.at[:, cols], wu_buf.at[slot], sem.at[1, slot]),
            pltpu.make_async_copy(wd_hbm.at[cols, :], wd_buf.at[slot], sem.at[2, slot]),
        )

    @pl.when(i == 0)
    def _():
        for cp in weight_copies(0, 0):
            cp.start()

    row_chunks = [pl.ds(r, NORM_ROWS) for r in range(0, x_ref.shape[0], NORM_ROWS)]
    for rows in row_chunks:
        x = x_ref[rows, :]
        h_ref[rows, :] = (x * _rms_scale(x) * n2pre_ref[...]).astype(BF16)
    o_ref[...] = jnp.zeros(o_ref.shape, F32)

    def hidden_tile(f, carry):
        it = i * nf + f
        slot = it % 2
        for cp in weight_copies(f, slot):
            cp.wait()

        @pl.when(it + 1 < n_tiles * nf)
        def _():
            for cp in weight_copies((f + 1) % nf, 1 - slot):
                cp.start()

        h = h_ref[...]
        g = jnp.dot(h, wg_buf[slot], preferred_element_type=F32)
        u = jnp.dot(h, wu_buf[slot], preferred_element_type=F32)
        hid = (g * _sigmoid(g) * u).astype(BF16)
        o_ref[...] += jnp.dot(hid, wd_buf[slot], preferred_element_type=F32)
        return carry

    lax.fori_loop(0, nf, hidden_tile, 0)

    for rows in row_chunks:
        y = o_ref[rows, :]
        o_ref[rows, :] = x_ref[rows, :] + y * _rms_scale(y) * n2post_ref[...]


def _ffn(x1, n2pre, wg, wu, wd, n2post):
    t, d = x1.shape
    d_ff = wg.shape[1]
    assert d_ff % TF_FFN == 0
    return pl.pallas_call(
        _ffn_kernel,
        grid=(t // TM_FFN,),
        in_specs=[
            pl.BlockSpec((TM_FFN, d), lambda i: (i, 0)),
            pl.BlockSpec((1, d), lambda i: (0, 0)),
            pl.BlockSpec(memory_space=pl.ANY),
            pl.BlockSpec(memory_space=pl.ANY),
            pl.BlockSpec(memory_space=pl.ANY),
            pl.BlockSpec((1, d), lambda i: (0, 0)),
        ],
        out_specs=pl.BlockSpec((TM_FFN, d), lambda i: (i, 0)),
        out_shape=jax.ShapeDtypeStruct((t, d), F32),
        scratch_shapes=[
            pltpu.VMEM((TM_FFN, d), BF16),
            pltpu.VMEM((2, d, TF_FFN), BF16),
            pltpu.VMEM((2, d, TF_FFN), BF16),
            pltpu.VMEM((2, TF_FFN, d), BF16),
            pltpu.SemaphoreType.DMA((3, 2)),
        ],
        compiler_params=pltpu.CompilerParams(
            dimension_semantics=("arbitrary",),
            vmem_limit_bytes=V7X_VMEM_LIMIT_BYTES),
        name="ffn",
    )(x1, n2pre, wg, wu, wd, n2post)


def kernel(x, norm1_pre, w_in, v_ln_g, v_ln_b, sgu_w, sgu_b, pool_w, pool_scale,
           w_a_out, w_b_out, w_mix_out, norm1_post, norm2_pre, w_ffn_gate,
           w_ffn_up, w_ffn_down, norm2_post):
    bsz, seq, d = x.shape
    depth = w_in.shape[0]
    d_sgu = v_ln_g.shape[1]
    d_pool = pool_scale.shape[1]
    assert seq % TM_IN == 0 and seq % TM_MIX == 0 and seq % TM_FFN == 0
    assert d_sgu == TN_IN and d_pool == TN_IN and d % TN_IN == 0
    assert TM_MIX % SGU_BLOCK == 0 and d_pool % len(POOL_WINDOWS) == 0

    x2 = x.reshape(bsz * seq, d)
    row = lambda a: a.reshape(1, -1)
    for l in range(depth):
        z3 = _inproj(x2, row(norm1_pre[l]), w_in[l].astype(BF16),
                     n_gelu=2 * d_sgu // TN_IN, n_plain=d_pool // TN_IN)
        x2, (wg, wu, wd) = _mix(
            z3, x2, row(v_ln_g[l]), row(v_ln_b[l]), sgu_w[l], sgu_b[l].T,
            pool_w[l].astype(BF16), row(pool_scale[l]),
            w_a_out[l].astype(BF16), w_b_out[l].astype(BF16),
            w_mix_out[l].astype(BF16), row(norm1_post[l]),
            [w_ffn_gate[l], w_ffn_up[l], w_ffn_down[l]], seq=seq)
        x2 = _ffn(x2, row(norm2_pre[l]), wg, wu, wd, row(norm2_post[l]))
    return x2.reshape(bsz, seq, d)
```
